```python
import math
import jax, jax.numpy as jnp
from jax import lax
import numpy as np

D_MODEL = 1024
BATCH = 1
SEQ = 16384
DEPTH = 1
DEC_BATCH = 128
DEC_SEQ = 4
PAST_LEN = 8192
PAGE_SIZE = 128

N_HEADS = 8
HEAD_DIM = 64
D_ATTN = N_HEADS * HEAD_DIM
D_CONV = D_MODEL // 2
CONV_WIDTH = 31
D_FF = 4 * D_MODEL
Q_BLOCK = 128
NORM_EPS = 1e-6
SB_BIAS_INIT = -6.0
D_IN = 2 * D_CONV + 3 * D_ATTN + 2 * D_MODEL

kernel_name = "hybrid_conformer_stickbreaking_step"


def _rmsnorm(x, g):
    xf = x.astype(jnp.float32)
    y = xf * lax.rsqrt(jnp.mean(xf * xf, axis=-1, keepdims=True) + NORM_EPS)
    return (y * g.astype(jnp.float32)).astype(x.dtype)


def _layernorm(x, g, b):
    xf = x.astype(jnp.float32)
    mu = jnp.mean(xf, axis=-1, keepdims=True)
    var = jnp.mean(jnp.square(xf - mu), axis=-1, keepdims=True)
    y = (xf - mu) * lax.rsqrt(var + NORM_EPS)
    return (y * g.astype(jnp.float32) + b.astype(jnp.float32)).astype(x.dtype)


def _conv_mixer(a, b, left, w_dw, b_dw, ln_g, ln_b, w_conv_out):
    u = a * jax.nn.sigmoid(b)
    full = jnp.concatenate([left.astype(u.dtype), u], axis=1)
    y = lax.conv_general_dilated(full, w_dw[:, None, :].astype(u.dtype), window_strides=(1,),
                                 padding='VALID', dimension_numbers=('NWC', 'WIO', 'NWC'),
                                 feature_group_count=D_CONV) + b_dw
    y = jax.nn.silu(_layernorm(y, ln_g, ln_b))
    return y @ w_conv_out, full[:, -(CONV_WIDTH - 1):]


def _sb_weights(z, q_pos, k_pos):
    z = z.astype(jnp.float32)
    mask = k_pos[None, :] < q_pos[:, None]
    log_1m = jnp.where(mask, jax.nn.log_sigmoid(-z), 0.0)
    suffix = lax.cumsum(log_1m, axis=z.ndim - 1, reverse=True) - log_1m
    return jnp.where(mask, jnp.exp(jax.nn.log_sigmoid(z) + suffix), 0.0)


def _sb_attention_prompt(q, k, v, sb_bias):
    B, T, H, Dh = q.shape
    nb = T // Q_BLOCK
    scale = 1.0 / math.sqrt(Dh)
    k_pos = jnp.arange(T)
    bias = sb_bias.astype(jnp.float32)[None, :, None, None]
    qb = q.reshape(B, nb, Q_BLOCK, H, Dh).transpose(1, 0, 2, 3, 4)

    def one_block(args):
        q_blk, blk = args
        q_pos = blk * Q_BLOCK + jnp.arange(Q_BLOCK)
        z = jnp.einsum('bqhd,bkhd->bhqk', q_blk, k, preferred_element_type=jnp.float32) * scale + bias
        a = _sb_weights(z, q_pos, k_pos)
        return jnp.einsum('bhqk,bkhd->bqhd', a.astype(v.dtype), v)

    o = lax.map(one_block, (qb, jnp.arange(nb)))
    return o.transpose(1, 0, 2, 3, 4).reshape(B, T, H * Dh)


def _sb_attention_sample(q, k_new, v_new, sb_bias, cache_k, cache_v, page_table):
    B, T, H, Dh = q.shape
    past = page_table.shape[1] * cache_k.shape[1]
    scale = 1.0 / math.sqrt(Dh)
    bias = sb_bias.astype(jnp.float32)[None, :, None, None]
    k_past = cache_k[page_table].reshape(B, past, H, Dh)
    v_past = cache_v[page_table].reshape(B, past, H, Dh)
    z = jnp.concatenate([
        jnp.einsum('bqhd,bkhd->bhqk', q, k_past.astype(q.dtype), preferred_element_type=jnp.float32),
        jnp.einsum('bqhd,bkhd->bhqk', q, k_new, preferred_element_type=jnp.float32)], axis=-1) * scale + bias
    q_pos = past + jnp.arange(T)
    k_pos = jnp.arange(past + T)
    a = _sb_weights(z, q_pos, k_pos).astype(v_new.dtype)
    o = (jnp.einsum('bhqk,bkhd->bqhd', a[..., :past], v_past.astype(v_new.dtype))
         + jnp.einsum('bhqk,bkhd->bqhd', a[..., past:], v_new))
    return o.reshape(B, T, H * Dh)


def _layer(x, conv_left, attend, g_pre_mix, w_in, sb_bias, w_dw, b_dw, ln_g, ln_b, w_conv_out,
           w_attn_out, w_o, g_post_mix, g_pre_ffn, w_ff1, w_ff2, g_post_ffn):
    B, T, _ = x.shape
    h = _rmsnorm(x, g_pre_mix)
    u = h @ w_in
    ca, cb, q, k, v, gc, ga = jnp.split(
        u, [D_CONV, 2 * D_CONV, 2 * D_CONV + D_ATTN, 2 * D_CONV + 2 * D_ATTN,
            2 * D_CONV + 3 * D_ATTN, 2 * D_CONV + 3 * D_ATTN + D_MODEL], axis=-1)
    conv_out, new_buf = _conv_mixer(ca, cb, conv_left, w_dw, b_dw, ln_g, ln_b, w_conv_out)
    q = q.reshape(B, T, N_HEADS, HEAD_DIM)
    k = k.reshape(B, T, N_HEADS, HEAD_DIM)
    v = v.reshape(B, T, N_HEADS, HEAD_DIM)
    attn_out = attend(q, k, v, sb_bias) @ w_attn_out
    mix = jax.nn.sigmoid(gc) * conv_out + jax.nn.sigmoid(ga) * attn_out
    x = x + _rmsnorm(mix @ w_o, g_post_mix)
    h2 = _rmsnorm(x, g_pre_ffn)
    f = jnp.square(jax.nn.relu(h2 @ w_ff1)) @ w_ff2
    x = x + _rmsnorm(f, g_post_ffn)
    return x, k, v, new_buf


def setup_inputs(seed: int = 0) -> dict:
    key = jax.random.key(seed)
    ks = jax.random.split(key, 21)
    n_pages = PAST_LEN // PAGE_SIZE
    n_used = DEC_BATCH * n_pages
    n_pool = (n_used * 5) // 4
    nrm = lambda k, shape, s: jax.random.normal(k, shape, jnp.float32) * s
    perm = jax.random.permutation(ks[5], n_pool)[:n_used]
    page_table = perm.reshape(DEC_BATCH, n_pages).astype(jnp.int32)
    return {
        "x_prompt": nrm(ks[0], (BATCH, SEQ, D_MODEL), 1.0),
        "x_sample": nrm(ks[1], (DEC_BATCH, DEC_SEQ, D_MODEL), 1.0),
        "cache_k": nrm(ks[2], (DEPTH, n_pool, PAGE_SIZE, N_HEADS, HEAD_DIM), 1.0),
        "cache_v": nrm(ks[3], (DEPTH, n_pool, PAGE_SIZE, N_HEADS, HEAD_DIM), 1.0),
        "state_conv": nrm(ks[4], (DEPTH, DEC_BATCH, CONV_WIDTH - 1, D_CONV), 1.0),
        "page_table": page_table,
        "g_pre_mix": 1.0 + nrm(ks[6], (DEPTH, D_MODEL), 0.02),
        "w_in": nrm(ks[7], (DEPTH, D_MODEL, D_IN), D_MODEL ** -0.5),
        "sb_bias": SB_BIAS_INIT + nrm(ks[20], (DEPTH, N_HEADS), 0.1),
        "w_dw": nrm(ks[8], (DEPTH, CONV_WIDTH, D_CONV), CONV_WIDTH ** -0.5),
        "b_dw": nrm(ks[9], (DEPTH, D_CONV), 0.02),
        "ln_g": 1.0 + nrm(ks[10], (DEPTH, D_CONV), 0.02),
        "ln_b": nrm(ks[11], (DEPTH, D_CONV), 0.02),
        "w_conv_out": nrm(ks[12], (DEPTH, D_CONV, D_MODEL), D_CONV ** -0.5),
        "w_attn_out": nrm(ks[13], (DEPTH, D_ATTN, D_MODEL), D_ATTN ** -0.5),
        "w_o": nrm(ks[14], (DEPTH, D_MODEL, D_MODEL), D_MODEL ** -0.5),
        "g_post_mix": 1.0 + nrm(ks[15], (DEPTH, D_MODEL), 0.02),
        "g_pre_ffn": 1.0 + nrm(ks[16], (DEPTH, D_MODEL), 0.02),
        "w_ff1": nrm(ks[17], (DEPTH, D_MODEL, D_FF), D_MODEL ** -0.5),
        "w_ff2": nrm(ks[18], (DEPTH, D_FF, D_MODEL), D_FF ** -0.5),
        "g_post_ffn": 1.0 + nrm(ks[19], (DEPTH, D_MODEL), 0.02),
    }


def reference(x_prompt, x_sample, cache_k, cache_v, state_conv, page_table,
              g_pre_mix, w_in, sb_bias, w_dw, b_dw, ln_g, ln_b, w_conv_out, w_attn_out, w_o,
              g_post_mix, g_pre_ffn, w_ff1, w_ff2, g_post_ffn):
    yp, ys = x_prompt, x_sample
    kp_l, vp_l, cp_l, ks_l, vs_l, cs_l = [], [], [], [], [], []
    for l in range(DEPTH):
        w = (g_pre_mix[l], w_in[l], sb_bias[l], w_dw[l], b_dw[l], ln_g[l], ln_b[l], w_conv_out[l],
             w_attn_out[l], w_o[l], g_post_mix[l], g_pre_ffn[l], w_ff1[l], w_ff2[l], g_post_ffn[l])
        left_p = jnp.zeros((yp.shape[0], CONV_WIDTH - 1, D_CONV), yp.dtype)
        yp, kp, vp, cp = _layer(yp, left_p, _sb_attention_prompt, *w)
        ck, cv = cache_k[l], cache_v[l]
        attend_s = lambda q, k, v, bb, ck=ck, cv=cv: _sb_attention_sample(q, k, v, bb, ck, cv, page_table)
        ys, ksm, vsm, csm = _layer(ys, state_conv[l], attend_s, *w)
        kp_l.append(kp); vp_l.append(vp); cp_l.append(cp)
        ks_l.append(ksm); vs_l.append(vsm); cs_l.append(csm)
    k_prompt = jnp.stack(kp_l)
    v_prompt = jnp.stack(vp_l)
    conv_prompt = jnp.stack(cp_l)
    k_sample = jnp.stack(ks_l)
    v_sample = jnp.stack(vs_l)
    conv_sample = jnp.stack(cs_l)
    return (yp, ys, k_prompt, v_prompt, conv_prompt, k_sample, v_sample, conv_sample)
```

```python
import functools
import math

import jax
import jax.numpy as jnp
from jax import lax
from jax.experimental import pallas as pl
from jax.experimental.pallas import tpu as pltpu

N_HEADS = 8
HEAD_DIM = 64
D_ATTN = N_HEADS * HEAD_DIM
CONV_WIDTH = 31
NORM_EPS = 1e-6

LANES = 128
HEAD_SLOT = LANES
N_BIAS_COLS = 3
VMEM_LIMIT = 56 * 1024 * 1024

_NT = (((1,), (1,)), ((), ()))


def _rmsnorm(x, g):
    return x * lax.rsqrt(jnp.mean(x * x, axis=-1, keepdims=True) + NORM_EPS) * g


def _softplus(z):
    return jnp.maximum(z, 0.0) + jnp.log(1.0 + jnp.exp(-jnp.abs(z)))


def _const_spec(shape):
    return pl.BlockSpec(shape, lambda *_: (0,) * len(shape), pipeline_mode=pl.Buffered(1))


def _proj_kernel(x_ref, g_ref, w_ref, qpat_ref, kpat_ref,
                 u_ref, qe_ref, ke_ref, k_ref, v_ref, vb_ref, gc_ref, ga_ref):
    d_conv = u_ref.shape[1]
    d_model = x_ref.shape[1]
    h = _rmsnorm(x_ref[...], g_ref[...]).astype(jnp.bfloat16)

    def mm(lo, width):
        return jnp.dot(h, w_ref[:, lo:lo + width], preferred_element_type=jnp.float32)

    r = mm(0, 2 * d_conv)
    u_ref[...] = r[:, :d_conv] * jax.nn.sigmoid(r[:, d_conv:])
    off = 2 * d_conv
    qe_ref[...] = (mm(off, N_HEADS * HEAD_SLOT) + qpat_ref[...]).astype(jnp.bfloat16)
    off += N_HEADS * HEAD_SLOT
    ke_ref[...] = (mm(off, N_HEADS * HEAD_SLOT) + kpat_ref[...]).astype(jnp.bfloat16)
    off += N_HEADS * HEAD_SLOT
    r = mm(off, 2 * D_ATTN)
    k_ref[...] = r[:, :D_ATTN]
    v_ref[...] = r[:, D_ATTN:]
    vb_ref[...] = r[:, D_ATTN:].astype(jnp.bfloat16)
    off += 2 * D_ATTN
    gc_ref[...] = jax.nn.sigmoid(mm(off, d_model))
    ga_ref[...] = jax.nn.sigmoid(mm(off + d_model, d_model))


def _proj(x, g, w_all, qpat, kpat, tm):
    rows, d_model = x.shape
    d_conv = d_model // 2
    n_w = w_all.shape[1]
    row = lambda n: pl.BlockSpec((tm, n), lambda i: (i, 0))
    f32, bf16 = jnp.float32, jnp.bfloat16
    return pl.pallas_call(
        _proj_kernel,
        grid=(rows // tm,),
        in_specs=[row(d_model), _const_spec((1, d_model)), _const_spec((d_model, n_w)),
                  _const_spec((1, N_HEADS * HEAD_SLOT)), _const_spec((1, N_HEADS * HEAD_SLOT))],
        out_specs=[row(d_conv), row(N_HEADS * HEAD_SLOT), row(N_HEADS * HEAD_SLOT),
                   row(D_ATTN), row(D_ATTN), row(D_ATTN), row(d_model), row(d_model)],
        out_shape=[jax.ShapeDtypeStruct((rows, d_conv), f32),
                   jax.ShapeDtypeStruct((rows, N_HEADS * HEAD_SLOT), bf16),
                   jax.ShapeDtypeStruct((rows, N_HEADS * HEAD_SLOT), bf16),
                   jax.ShapeDtypeStruct((rows, D_ATTN), f32),
                   jax.ShapeDtypeStruct((rows, D_ATTN), f32),
                   jax.ShapeDtypeStruct((rows, D_ATTN), bf16),
                   jax.ShapeDtypeStruct((rows, d_model), f32),
                   jax.ShapeDtypeStruct((rows, d_model), f32)],
        compiler_params=pltpu.CompilerParams(dimension_semantics=("arbitrary",),
                                             vmem_limit_bytes=VMEM_LIMIT),
        name="proj",
    )(x, g, w_all, qpat, kpat)


def _ln_silu(y, g, b):
    mu = jnp.mean(y, axis=-1, keepdims=True)
    d = y - mu
    var = jnp.mean(d * d, axis=-1, keepdims=True)
    n = d * lax.rsqrt(var + NORM_EPS) * g + b
    return n * jax.nn.sigmoid(n)


HALO = 32


def _conv_prompt_kernel(prev_ref, cur_ref, w_ref, b_ref, g_ref, lb_ref, c_ref, full_ref):
    tm = cur_ref.shape[0]
    i = pl.program_id(0)
    full_ref[0:HALO, :] = jnp.where(i > 0, prev_ref[...], 0.0)
    full_ref[HALO:, :] = cur_ref[...]
    first = HALO - (CONV_WIDTH - 1)
    acc = jnp.broadcast_to(b_ref[...], (tm, cur_ref.shape[1]))
    for j in range(CONV_WIDTH):
        acc = acc + full_ref[first + j:first + j + tm, :] * w_ref[j:j + 1, :]
    c_ref[...] = _ln_silu(acc, g_ref[...], lb_ref[...]).astype(jnp.bfloat16)


def _conv_prompt(u, w_dw, b_dw, ln_g, ln_b, tm):
    rows, d = u.shape
    per = tm // HALO
    return pl.pallas_call(
        _conv_prompt_kernel,
        grid=(rows // tm,),
        in_specs=[pl.BlockSpec((HALO, d), lambda i: (jnp.maximum(i * per - 1, 0), 0)),
                  pl.BlockSpec((tm, d), lambda i: (i, 0)),
                  _const_spec(w_dw.shape), _const_spec((1, d)), _const_spec((1, d)), _const_spec((1, d))],
        out_specs=pl.BlockSpec((tm, d), lambda i: (i, 0)),
        out_shape=jax.ShapeDtypeStruct((rows, d), jnp.bfloat16),
        scratch_shapes=[pltpu.VMEM((tm + HALO, d), jnp.float32)],
        compiler_params=pltpu.CompilerParams(dimension_semantics=("arbitrary",)),
        name="conv_prompt",
    )(u, u, w_dw, b_dw, ln_g, ln_b)


def _conv_sample_kernel(state_ref, u_ref, w_ref, b_ref, g_ref, lb_ref, c_ref):
    n_state = state_ref.shape[0]
    t_new, nb, d = u_ref.shape
    for t in range(t_new):
        acc = jnp.broadcast_to(b_ref[...], (nb, d))
        for j in range(CONV_WIDTH):
            r = t + j
            src = state_ref[r] if r < n_state else u_ref[r - n_state]
            acc = acc + src * w_ref[j:j + 1, :]
        c_ref[t] = _ln_silu(acc, g_ref[...], lb_ref[...]).astype(jnp.bfloat16)


def _conv_sample(state_t, u_t, w_dw, b_dw, ln_g, ln_b):
    t_new, nb, d = u_t.shape
    return pl.pallas_call(
        _conv_sample_kernel,
        out_shape=jax.ShapeDtypeStruct((t_new, nb, d), jnp.bfloat16),
        compiler_params=pltpu.CompilerParams(vmem_limit_bytes=VMEM_LIMIT),
        name="conv_sample",
    )(state_t, u_t, w_dw, b_dw, ln_g, ln_b)


def _sb_tile(z, tri, valid):
    sp = _softplus(z)
    if valid is not None:
        sp = jnp.where(valid, sp, 0.0)
    hi = sp.astype(jnp.bfloat16)
    lo = (sp - hi.astype(jnp.float32)).astype(jnp.bfloat16)
    s_incl = (jnp.dot(hi, tri, preferred_element_type=jnp.float32)
              + jnp.dot(lo, tri, preferred_element_type=jnp.float32))
    log_a = z - s_incl
    if valid is not None:
        log_a = jnp.where(valid, log_a, -jnp.inf)
    return jnp.exp(log_a), s_incl[:, 0:1]


def _attn_prompt_kernel(q_ref, k_ref, v_ref, tri_ref, o_ref, acc_ref, carry_ref, *, tq, tk):
    i = pl.program_id(1)
    n_diag = tq // tk
    heads_per_step = q_ref.shape[1] // HEAD_SLOT
    outs = []
    for hh in range(heads_per_step):
        lanes = slice(hh * HEAD_SLOT, (hh + 1) * HEAD_SLOT)
        q = q_ref[:, lanes]
        acc_ref[...] = jnp.zeros_like(acc_ref)
        carry_ref[...] = jnp.zeros_like(carry_ref)

        def tile(j, valid):
            start = pl.multiple_of(j * tk, tk)
            z = lax.dot_general(q, k_ref[pl.ds(start, tk), lanes], _NT,
                                preferred_element_type=jnp.float32)
            a, total = _sb_tile(z, tri_ref[...], valid)
            pv = jnp.dot(a.astype(jnp.bfloat16), v_ref[pl.ds(start, tk), :],
                         preferred_element_type=jnp.float32)
            carry = carry_ref[...]
            acc_ref[...] += jnp.exp(-carry) * pv
            carry_ref[...] = carry + total

        row = lax.broadcasted_iota(jnp.int32, (tq, tk), 0)
        col = lax.broadcasted_iota(jnp.int32, (tq, tk), 1)
        for d in reversed(range(n_diag)):
            tile(i * n_diag + d, col + d * tk < row)

        def body(s, _):
            tile(i * n_diag - 1 - s, None)
            return 0

        lax.fori_loop(0, i * n_diag, body, 0)
        outs.append(acc_ref[...])
    lane = lax.broadcasted_iota(jnp.int32, outs[0].shape, 1)
    o = outs[0]
    for hh in range(1, heads_per_step):
        o = jnp.where(lane >= hh * HEAD_DIM, outs[hh], o)
    o_ref[...] = o.astype(o_ref.dtype)


def _attn_prompt(qe, ke, vb, tri, tq, tk):
    t = qe.shape[0]
    hp = 2
    n_groups = N_HEADS // hp
    kern = functools.partial(_attn_prompt_kernel, tq=tq, tk=tk)
    return pl.pallas_call(
        kern,
        grid=(n_groups, t // tq),
        in_specs=[pl.BlockSpec((tq, hp * HEAD_SLOT), lambda g, i: (i, g)),
                  pl.BlockSpec((t, hp * HEAD_SLOT), lambda g, i: (0, g)),
                  pl.BlockSpec((t, hp * HEAD_DIM), lambda g, i: (0, g)),
                  _const_spec((tk, tk))],
        out_specs=pl.BlockSpec((tq, hp * HEAD_DIM), lambda g, i: (i, g)),
        out_shape=jax.ShapeDtypeStruct((t, D_ATTN), jnp.bfloat16),
        scratch_shapes=[pltpu.VMEM((tq, hp * HEAD_DIM), jnp.float32),
                        pltpu.VMEM((tq, 1), jnp.float32)],
        compiler_params=pltpu.CompilerParams(dimension_semantics=("arbitrary", "arbitrary"),
                                             vmem_limit_bytes=VMEM_LIMIT),
        name="attn_prompt",
    )(qe, ke, vb, tri)


PAGES_PER_CHUNK = 8
N_SLOTS = 3
NEW_PAD = LANES


def _attn_sample_kernel(pt_ref, q_ref, kn_ref, vn_ref, bias_ref, tri_ref, ck_hbm, cv_hbm,
                        o_ref, kbuf, vbuf, sem, acc_ref, carry_ref, *, n_pages, page, tk):
    b = pl.program_id(0)
    nb = pl.num_programs(0)
    n_chunks = n_pages // PAGES_PER_CHUNK
    total = nb * n_chunks
    t_new = q_ref.shape[1]
    rows = t_new * N_HEADS
    d = q_ref.shape[2]

    def copies(g, slot):
        bb = g // n_chunks
        c = n_chunks - 1 - g % n_chunks
        out = []
        for p in range(PAGES_PER_CHUNK):
            pid = pt_ref[bb * n_pages + c * PAGES_PER_CHUNK + p]
            dst = pl.ds(p * page, page)
            out.append(pltpu.make_async_copy(ck_hbm.at[pid], kbuf.at[slot, dst, :], sem.at[slot, 0]))
            out.append(pltpu.make_async_copy(cv_hbm.at[pid], vbuf.at[slot, dst, :], sem.at[slot, 1]))
        return out

    def start(g):
        for cp in copies(g, g % N_SLOTS):
            cp.start()

    @pl.when(b == 0)
    def _():
        for g0 in range(N_SLOTS - 1):
            start(g0)

    q4 = q_ref[0]
    qb = jnp.concatenate([jnp.broadcast_to(q4[t:t + 1, :], (N_HEADS, d)) for t in range(t_new)], axis=0)
    r_id = lax.broadcasted_iota(jnp.int32, (rows, d), 0)
    l_id = lax.broadcasted_iota(jnp.int32, (rows, d), 1)
    head_mask = (l_id // HEAD_DIM) == (r_id % N_HEADS)
    qb = jnp.where(head_mask, qb, 0.0).astype(jnp.bfloat16)
    bias = bias_ref[...][:, 0:1]

    def accumulate(a, total_sp, v):
        pv = jnp.dot(a.astype(jnp.bfloat16), v, preferred_element_type=jnp.float32)
        carry = carry_ref[...]
        acc_ref[...] += jnp.exp(-carry) * pv
        carry_ref[...] = carry + total_sp

    acc_ref[...] = jnp.zeros_like(acc_ref)
    carry_ref[...] = jnp.zeros_like(carry_ref)

    pad = jnp.zeros((NEW_PAD - t_new, d), jnp.float32)
    kn = jnp.concatenate([kn_ref[0], pad], axis=0).astype(jnp.bfloat16)
    vn = jnp.concatenate([vn_ref[0], pad], axis=0).astype(jnp.bfloat16)
    z = lax.dot_general(qb, kn, _NT, preferred_element_type=jnp.float32) + bias
    rr = lax.broadcasted_iota(jnp.int32, (rows, NEW_PAD), 0)
    cc = lax.broadcasted_iota(jnp.int32, (rows, NEW_PAD), 1)
    a, tot = _sb_tile(z, tri_ref[:NEW_PAD, :NEW_PAD], cc < rr // N_HEADS)
    accumulate(a, tot, vn)

    def chunk(cidx, _):
        g = b * n_chunks + cidx
        slot = g % N_SLOTS
        for cp in copies(g, slot):
            cp.wait()

        @pl.when(g + N_SLOTS - 1 < total)
        def _():
            start(g + N_SLOTS - 1)

        for s in reversed(range(PAGES_PER_CHUNK * page // tk)):
            kc = kbuf[slot, s * tk:(s + 1) * tk, :].astype(jnp.bfloat16)
            vc = vbuf[slot, s * tk:(s + 1) * tk, :].astype(jnp.bfloat16)
            zz = lax.dot_general(qb, kc, _NT, preferred_element_type=jnp.float32) + bias
            aa, tt = _sb_tile(zz, tri_ref[...], None)
            accumulate(aa, tt, vc)
        return 0

    lax.fori_loop(0, n_chunks, chunk, 0)

    o_full = jnp.where(head_mask, acc_ref[...], 0.0)
    o_ref[0] = jnp.sum(o_full.reshape(t_new, N_HEADS, d), axis=1).astype(o_ref.dtype)


def _attn_sample(page_table, q_s, kn_s, vn_s, bias_rows, tri, cache_k, cache_v, tk):
    nb, t_new, d = q_s.shape
    n_pages = page_table.shape[1]
    n_pool, page, _ = cache_k.shape
    rows = t_new * N_HEADS
    kern = functools.partial(_attn_sample_kernel, n_pages=n_pages, page=page, tk=tk)
    per_b = lambda shape: pl.BlockSpec(shape, lambda b, pt: (b, 0, 0))
    const = lambda shape: pl.BlockSpec(shape, lambda b, pt: (0,) * len(shape))
    grid_spec = pltpu.PrefetchScalarGridSpec(
        num_scalar_prefetch=1,
        grid=(nb,),
        in_specs=[per_b((1, t_new, d)), per_b((1, t_new, d)), per_b((1, t_new, d)),
                  const((rows, LANES)), const((tk, tk)),
                  pl.BlockSpec(memory_space=pl.ANY), pl.BlockSpec(memory_space=pl.ANY)],
        out_specs=per_b((1, t_new, d)),
        scratch_shapes=[pltpu.VMEM((N_SLOTS, PAGES_PER_CHUNK * page, d), jnp.float32),
                        pltpu.VMEM((N_SLOTS, PAGES_PER_CHUNK * page, d), jnp.float32),
                        pltpu.SemaphoreType.DMA((N_SLOTS, 2)),
                        pltpu.VMEM((rows, d), jnp.float32),
                        pltpu.VMEM((rows, 1), jnp.float32)],
    )
    return pl.pallas_call(
        kern,
        grid_spec=grid_spec,
        out_shape=jax.ShapeDtypeStruct((nb, t_new, d), jnp.bfloat16),
        compiler_params=pltpu.CompilerParams(dimension_semantics=("arbitrary",),
                                             vmem_limit_bytes=VMEM_LIMIT),
        name="attn_sample",
    )(page_table.reshape(-1), q_s, kn_s, vn_s, bias_rows, tri, cache_k, cache_v)


FF_CHUNK = 1024


def _post_kernel(x_ref, c_ref, o_ref, gc_ref, ga_ref, wc_ref, wa_ref, wo_ref, w1_ref, w2_ref,
                 g1_ref, g2_ref, g3_ref, y_ref):
    f32, bf16 = jnp.float32, jnp.bfloat16
    conv_out = jnp.dot(c_ref[...], wc_ref[...], preferred_element_type=f32)
    attn_out = jnp.dot(o_ref[...], wa_ref[...], preferred_element_type=f32)
    mix = gc_ref[...] * conv_out + ga_ref[...] * attn_out
    m = jnp.dot(mix.astype(bf16), wo_ref[...], preferred_element_type=f32)
    x1 = x_ref[...] + _rmsnorm(m, g1_ref[...])
    h2 = _rmsnorm(x1, g2_ref[...]).astype(bf16)
    f = jnp.zeros_like(x1)
    for c in range(0, w1_ref.shape[1], FF_CHUNK):
        hid = jnp.maximum(jnp.dot(h2, w1_ref[:, c:c + FF_CHUNK], preferred_element_type=f32), 0.0)
        f = f + jnp.dot((hid * hid).astype(bf16), w2_ref[c:c + FF_CHUNK, :], preferred_element_type=f32)
    y_ref[...] = x1 + _rmsnorm(f, g3_ref[...])


def _post(x, c, o, gc, ga, wc, wa, wo, w1, w2, g1, g2, g3, tm):
    rows, d_model = x.shape
    row = lambda n: pl.BlockSpec((tm, n), lambda i: (i, 0))
    return pl.pallas_call(
        _post_kernel,
        grid=(rows // tm,),
        in_specs=[row(d_model), row(c.shape[1]), row(o.shape[1]), row(d_model), row(d_model),
                  _const_spec(wc.shape), _const_spec(wa.shape), _const_spec(wo.shape),
                  _const_spec(w1.shape), _const_spec(w2.shape),
                  _const_spec((1, d_model)), _const_spec((1, d_model)), _const_spec((1, d_model))],
        out_specs=row(d_model),
        out_shape=jax.ShapeDtypeStruct((rows, d_model), jnp.float32),
        compiler_params=pltpu.CompilerParams(dimension_semantics=("arbitrary",),
                                             vmem_limit_bytes=VMEM_LIMIT),
        name="post",
    )(x, c, o, gc, ga, wc, wa, wo, w1, w2, g1, g2, g3)


def _head_slots(w):
    k = w.shape[0]
    w = w.reshape(k, N_HEADS, HEAD_DIM)
    w = jnp.pad(w, ((0, 0), (0, 0), (0, HEAD_SLOT - HEAD_DIM)))
    return w.reshape(k, N_HEADS * HEAD_SLOT)


def _bias_patterns(sb_bias):
    b = sb_bias.astype(jnp.float32)
    terms = []
    rem = b
    for _ in range(N_BIAS_COLS):
        t = rem.astype(jnp.bfloat16).astype(jnp.float32)
        terms.append(t)
        rem = rem - t
    qpat = jnp.zeros((N_HEADS, HEAD_SLOT), jnp.float32)
    kpat = jnp.zeros((N_HEADS, HEAD_SLOT), jnp.float32)
    for n, t in enumerate(terms):
        qpat = qpat.at[:, HEAD_DIM + n].set(t)
        kpat = kpat.at[:, HEAD_DIM + n].set(1.0)
    return qpat.reshape(1, -1), kpat.reshape(1, -1)


def _pick_tile(rows, pref):
    t = min(pref, rows)
    while rows % t:
        t //= 2
    return t


def kernel(x_prompt, x_sample, cache_k, cache_v, state_conv, page_table, g_pre_mix, w_in, sb_bias, w_dw,
           b_dw, ln_g, ln_b, w_conv_out, w_attn_out, w_o, g_post_mix, g_pre_ffn, w_ff1, w_ff2, g_post_ffn):
    depth = w_in.shape[0]
    assert depth == 1, "single-layer step"
    f32, bf16 = jnp.float32, jnp.bfloat16
    batch, seq, d_model = x_prompt.shape
    assert batch == 1
    dec_batch, dec_seq, _ = x_sample.shape
    d_conv = d_model // 2
    n_pool, page = cache_k.shape[1], cache_k.shape[2]
    scale = 1.0 / math.sqrt(HEAD_DIM)

    w = w_in[0]
    o_q = 2 * d_conv
    w_q = w[:, o_q:o_q + D_ATTN]
    w_k = w[:, o_q + D_ATTN:o_q + 2 * D_ATTN]
    w_all = jnp.concatenate(
        [w[:, :o_q], _head_slots(w_q * scale), _head_slots(w_k),
         w[:, o_q + D_ATTN:o_q + 3 * D_ATTN], w[:, o_q + 3 * D_ATTN:]], axis=1).astype(bf16)
    qpat, kpat = _bias_patterns(sb_bias[0])
    row1 = lambda a: a[0].reshape(1, -1).astype(f32)
    g0, g1, g2, g3 = row1(g_pre_mix), row1(g_post_mix), row1(g_pre_ffn), row1(g_post_ffn)
    wdw = w_dw[0].astype(f32)
    bdw, lng, lnb = row1(b_dw), row1(ln_g), row1(ln_b)
    wc, wa, wo = w_conv_out[0].astype(bf16), w_attn_out[0].astype(bf16), w_o[0].astype(bf16)
    w1, w2 = w_ff1[0].astype(bf16), w_ff2[0].astype(bf16)

    tk = 256
    jj = lax.broadcasted_iota(jnp.int32, (tk, tk), 0)
    ss = lax.broadcasted_iota(jnp.int32, (tk, tk), 1)
    tri = (jj >= ss).astype(bf16)

    xp = x_prompt[0]
    tm = _pick_tile(seq, 256)
    u_p, qe_p, ke_p, k_p, v_p, vb_p, gc_p, ga_p = _proj(xp, g0, w_all, qpat, kpat, tm)
    c_p = _conv_prompt(u_p, wdw, bdw, lng, lnb, _pick_tile(seq, 512))
    o_p = _attn_prompt(qe_p, ke_p, vb_p, tri, tq=_pick_tile(seq, 256), tk=tk)
    y_p = _post(xp, c_p, o_p, gc_p, ga_p, wc, wa, wo, w1, w2, g1, g2, g3, tm)

    xs = x_sample.reshape(dec_batch * dec_seq, d_model)
    tms = _pick_tile(dec_batch * dec_seq, 256)
    u_s, qe_s, ke_s, k_s, v_s, vb_s, gc_s, ga_s = _proj(xs, g0, w_all, qpat, kpat, tms)
    state_t = jnp.transpose(state_conv[0], (1, 0, 2))
    u_t = jnp.transpose(u_s.reshape(dec_batch, dec_seq, d_conv), (1, 0, 2))
    c_t = _conv_sample(state_t, u_t, wdw, bdw, lng, lnb)
    c_s = jnp.transpose(c_t, (1, 0, 2)).reshape(dec_batch * dec_seq, d_conv)
    unslot = lambda a: a.reshape(-1, N_HEADS, HEAD_SLOT)[:, :, :HEAD_DIM].reshape(dec_batch, dec_seq, D_ATTN)
    q_s = unslot(qe_s).astype(f32)
    kn_s = unslot(ke_s).astype(f32)
    vn_s = vb_s.reshape(dec_batch, dec_seq, D_ATTN).astype(f32)
    bias_rows = jnp.broadcast_to(jnp.tile(sb_bias[0].astype(f32), dec_seq)[:, None],
                                 (dec_seq * N_HEADS, LANES))
    o_s = _attn_sample(page_table, q_s, kn_s, vn_s, bias_rows, tri,
                       cache_k[0].reshape(n_pool, page, D_ATTN), cache_v[0].reshape(n_pool, page, D_ATTN), tk)
    y_s = _post(xs, c_s, o_s.reshape(dec_batch * dec_seq, D_ATTN), gc_s, ga_s,
                wc, wa, wo, w1, w2, g1, g2, g3, tms)

    n_state = CONV_WIDTH - 1
    y_prompt = y_p.reshape(batch, seq, d_model)
    y_sample = y_s.reshape(dec_batch, dec_seq, d_model)
    k_prompt = k_p.reshape(1, batch, seq, N_HEADS, HEAD_DIM)
    v_prompt = v_p.reshape(1, batch, seq, N_HEADS, HEAD_DIM)
    conv_prompt = u_p[seq - n_state:].reshape(1, batch, n_state, d_conv)
    k_sample = k_s.reshape(1, dec_batch, dec_seq, N_HEADS, HEAD_DIM)
    v_sample = v_s.reshape(1, dec_batch, dec_seq, N_HEADS, HEAD_DIM)
    conv_sample = jnp.concatenate(
        [state_conv[0][:, dec_seq:, :], u_s.reshape(dec_batch, dec_seq, d_conv)], axis=1)[None]
    return (y_prompt, y_sample, k_prompt, v_prompt, conv_prompt, k_sample, v_sample, conv_sample)
```

```python
import functools
import math

import jax
import jax.numpy as jnp
from jax import lax
from jax.experimental import pallas as pl
from jax.experimental.pallas import tpu as pltpu

N_HEADS = 8
HEAD_DIM = 64
D_ATTN = N_HEADS * HEAD_DIM
CONV_WIDTH = 31
NORM_EPS = 1e-6
LOG2E = math.log2(math.e)

LANES = 128
HEAD_SLOT = LANES
N_BIAS_COLS = 3
VMEM_LIMIT = 56 * 1024 * 1024

_NT = (((1,), (1,)), ((), ()))
_NN = (((1,), (0,)), ((), ()))


def _rmsnorm(x, g):
    return x * lax.rsqrt(jnp.mean(x * x, axis=-1, keepdims=True) + NORM_EPS) * g


def _softplus2(z):
    sign_bit = jnp.uint32(0x80000000)
    neg_abs = lax.bitcast_convert_type(lax.bitcast_convert_type(z, jnp.uint32) | sign_bit, jnp.float32)
    return jnp.maximum(z, 0.0) + jnp.log2(1.0 + jnp.exp2(neg_abs))


def _const_spec(shape):
    return pl.BlockSpec(shape, lambda *_: (0,) * len(shape), pipeline_mode=pl.Buffered(1))


def _proj_kernel(x_ref, g_ref, w_ref, qpat_ref, kpat_ref,
                 u_ref, qe_ref, ke_ref, k_ref, v_ref, vb_ref, gc_ref, ga_ref):
    d_conv = u_ref.shape[1]
    d_model = x_ref.shape[1]
    h = _rmsnorm(x_ref[...], g_ref[...]).astype(jnp.bfloat16)

    def mm(lo, width):
        return jnp.dot(h, w_ref[:, lo:lo + width], preferred_element_type=jnp.float32)

    r = mm(0, 2 * d_conv)
    u_ref[...] = r[:, :d_conv] * jax.nn.sigmoid(r[:, d_conv:])
    off = 2 * d_conv
    qe_ref[...] = (mm(off, N_HEADS * HEAD_SLOT) + qpat_ref[...]).astype(jnp.bfloat16)
    off += N_HEADS * HEAD_SLOT
    ke_ref[...] = (mm(off, N_HEADS * HEAD_SLOT) + kpat_ref[...]).astype(jnp.bfloat16)
    off += N_HEADS * HEAD_SLOT
    r = mm(off, 2 * D_ATTN)
    k_ref[...] = r[:, :D_ATTN]
    v_ref[...] = r[:, D_ATTN:]
    vb_ref[...] = r[:, D_ATTN:].astype(jnp.bfloat16)
    off += 2 * D_ATTN
    gc_ref[...] = jax.nn.sigmoid(mm(off, d_model))
    ga_ref[...] = jax.nn.sigmoid(mm(off + d_model, d_model))


def _proj(x, g, w_all, qpat, kpat, tm):
    rows, d_model = x.shape
    d_conv = d_model // 2
    n_w = w_all.shape[1]
    row = lambda n: pl.BlockSpec((tm, n), lambda i: (i, 0))
    f32, bf16 = jnp.float32, jnp.bfloat16
    return pl.pallas_call(
        _proj_kernel,
        grid=(rows // tm,),
        in_specs=[row(d_model), _const_spec((1, d_model)), _const_spec((d_model, n_w)),
                  _const_spec((1, N_HEADS * HEAD_SLOT)), _const_spec((1, N_HEADS * HEAD_SLOT))],
        out_specs=[row(d_conv), row(N_HEADS * HEAD_SLOT), row(N_HEADS * HEAD_SLOT),
                   row(D_ATTN), row(D_ATTN), row(D_ATTN), row(d_model), row(d_model)],
        out_shape=[jax.ShapeDtypeStruct((rows, d_conv), f32),
                   jax.ShapeDtypeStruct((rows, N_HEADS * HEAD_SLOT), bf16),
                   jax.ShapeDtypeStruct((rows, N_HEADS * HEAD_SLOT), bf16),
                   jax.ShapeDtypeStruct((rows, D_ATTN), f32),
                   jax.ShapeDtypeStruct((rows, D_ATTN), f32),
                   jax.ShapeDtypeStruct((rows, D_ATTN), bf16),
                   jax.ShapeDtypeStruct((rows, d_model), f32),
                   jax.ShapeDtypeStruct((rows, d_model), f32)],
        compiler_params=pltpu.CompilerParams(dimension_semantics=("arbitrary",),
                                             vmem_limit_bytes=VMEM_LIMIT),
        name="proj",
    )(x, g, w_all, qpat, kpat)


def _ln_silu(y, g, b):
    mu = jnp.mean(y, axis=-1, keepdims=True)
    d = y - mu
    var = jnp.mean(d * d, axis=-1, keepdims=True)
    n = d * lax.rsqrt(var + NORM_EPS) * g + b
    return n * jax.nn.sigmoid(n)


HALO = 32


def _conv_prompt_kernel(prev_ref, cur_ref, w_ref, b_ref, g_ref, lb_ref, c_ref, full_ref):
    tm = cur_ref.shape[0]
    i = pl.program_id(0)
    full_ref[0:HALO, :] = jnp.where(i > 0, prev_ref[...], 0.0)
    full_ref[HALO:, :] = cur_ref[...]
    first = HALO - (CONV_WIDTH - 1)
    acc = jnp.broadcast_to(b_ref[...], (tm, cur_ref.shape[1]))
    for j in range(CONV_WIDTH):
        acc = acc + full_ref[first + j:first + j + tm, :] * w_ref[j:j + 1, :]
    c_ref[...] = _ln_silu(acc, g_ref[...], lb_ref[...]).astype(jnp.bfloat16)


def _conv_prompt(u, w_dw, b_dw, ln_g, ln_b, tm):
    rows, d = u.shape
    per = tm // HALO
    return pl.pallas_call(
        _conv_prompt_kernel,
        grid=(rows // tm,),
        in_specs=[pl.BlockSpec((HALO, d), lambda i: (jnp.maximum(i * per - 1, 0), 0)),
                  pl.BlockSpec((tm, d), lambda i: (i, 0)),
                  _const_spec(w_dw.shape), _const_spec((1, d)), _const_spec((1, d)), _const_spec((1, d))],
        out_specs=pl.BlockSpec((tm, d), lambda i: (i, 0)),
        out_shape=jax.ShapeDtypeStruct((rows, d), jnp.bfloat16),
        scratch_shapes=[pltpu.VMEM((tm + HALO, d), jnp.float32)],
        compiler_params=pltpu.CompilerParams(dimension_semantics=("arbitrary",)),
        name="conv_prompt",
    )(u, u, w_dw, b_dw, ln_g, ln_b)


def _conv_sample_kernel(state_ref, u_ref, w_ref, b_ref, g_ref, lb_ref, c_ref):
    n_state = state_ref.shape[0]
    t_new, nb, d = u_ref.shape
    for t in range(t_new):
        acc = jnp.broadcast_to(b_ref[...], (nb, d))
        for j in range(CONV_WIDTH):
            r = t + j
            src = state_ref[r] if r < n_state else u_ref[r - n_state]
            acc = acc + src * w_ref[j:j + 1, :]
        c_ref[t] = _ln_silu(acc, g_ref[...], lb_ref[...]).astype(jnp.bfloat16)


def _conv_sample(state_t, u_t, w_dw, b_dw, ln_g, ln_b):
    t_new, nb, d = u_t.shape
    return pl.pallas_call(
        _conv_sample_kernel,
        out_shape=jax.ShapeDtypeStruct((t_new, nb, d), jnp.bfloat16),
        compiler_params=pltpu.CompilerParams(vmem_limit_bytes=VMEM_LIMIT),
        name="conv_sample",
    )(state_t, u_t, w_dw, b_dw, ln_g, ln_b)


def _sb_masses(z, valid):
    sp = _softplus2(z)
    return sp if valid is None else jnp.where(valid, sp, 0.0)


def _sb_split(sp):
    hi = sp.astype(jnp.bfloat16)
    lo = (sp - hi.astype(jnp.float32)).astype(jnp.bfloat16)
    return jnp.concatenate([hi, lo], axis=1)


def _sb_suffix(hi_lo, tri2):
    return jnp.dot(hi_lo, tri2, preferred_element_type=jnp.float32)


def _sb_weights(z, s_incl, valid):
    log_a = z - s_incl
    if valid is not None:
        log_a = jnp.where(valid, log_a, -jnp.inf)
    return jnp.exp2(log_a).astype(jnp.bfloat16)


def _sb_tile(z, tri2, valid):
    s_incl = _sb_suffix(_sb_split(_sb_masses(z, valid)), tri2)
    return _sb_weights(z, s_incl, valid), s_incl[:, 0:1]


TILES_PER_STEP = 2


def _attn_prompt_kernel(q_ref, k_ref, v_ref, tri2_ref, o_ref, acc_ref, carry_ref, *, tq, tk):
    i = pl.program_id(1)
    n_diag = tq // tk
    heads = q_ref.shape[1] // HEAD_SLOT
    acc_ref[...] = jnp.zeros_like(acc_ref)
    carry_ref[...] = jnp.zeros_like(carry_ref)

    def step(js, valids):
        pairs = [(hh, j, valid) for j, valid in zip(js, valids) for hh in range(heads)]
        lanes = lambda hh: slice(hh * HEAD_SLOT, (hh + 1) * HEAD_SLOT)
        keys = lambda j: pl.ds(pl.multiple_of(j * tk, tk), tk)
        zs = [lax.dot_general(q_ref[:, lanes(hh)], k_ref[keys(j), lanes(hh)], _NT,
                              preferred_element_type=jnp.float32) for hh, j, _ in pairs]
        sps = [_sb_masses(z, valid) for z, (_, _, valid) in zip(zs, pairs)]
        splits = [_sb_split(sp) for sp in sps]
        sums = [_sb_suffix(hl, tri2_ref[...]) for hl in splits]
        weights = [_sb_weights(z, s, valid) for z, s, (_, _, valid) in zip(zs, sums, pairs)]
        pvs = [jnp.dot(a, v_ref[keys(j), :], preferred_element_type=jnp.float32)
               for a, (_, j, _) in zip(weights, pairs)]
        for hh in range(heads):
            acc = acc_ref[hh]
            carry = carry_ref[hh]
            for n, (h2, _, _) in enumerate(pairs):
                if h2 == hh:
                    acc = acc + jnp.exp2(-carry) * pvs[n]
                    carry = carry + sums[n][:, 0:1]
            acc_ref[hh] = acc
            carry_ref[hh] = carry

    row = lax.broadcasted_iota(jnp.int32, (tq, tk), 0)
    col = lax.broadcasted_iota(jnp.int32, (tq, tk), 1)
    diag = list(reversed(range(n_diag)))
    step([i * n_diag + d for d in diag], [col + d * tk < row for d in diag])

    n_full = i * n_diag
    assert n_diag % TILES_PER_STEP == 0

    def body(s, _):
        first = n_full - 1 - s * TILES_PER_STEP
        step([first - u for u in range(TILES_PER_STEP)], [None] * TILES_PER_STEP)
        return 0

    lax.fori_loop(0, n_full // TILES_PER_STEP, body, 0)

    lane = lax.broadcasted_iota(jnp.int32, (tq, heads * HEAD_DIM), 1)
    o = acc_ref[0]
    for hh in range(1, heads):
        o = jnp.where(lane >= hh * HEAD_DIM, acc_ref[hh], o)
    o_ref[...] = o.astype(o_ref.dtype)


def _attn_prompt(qe, ke, vb, tri2, tq, tk):
    t = qe.shape[0]
    hp = 2
    n_groups = N_HEADS // hp
    kern = functools.partial(_attn_prompt_kernel, tq=tq, tk=tk)
    return pl.pallas_call(
        kern,
        grid=(n_groups, t // tq),
        in_specs=[pl.BlockSpec((tq, hp * HEAD_SLOT), lambda g, i: (i, g)),
                  pl.BlockSpec((t, hp * HEAD_SLOT), lambda g, i: (0, g)),
                  pl.BlockSpec((t, hp * HEAD_DIM), lambda g, i: (0, g)),
                  _const_spec((2 * tk, tk))],
        out_specs=pl.BlockSpec((tq, hp * HEAD_DIM), lambda g, i: (i, g)),
        out_shape=jax.ShapeDtypeStruct((t, D_ATTN), jnp.bfloat16),
        scratch_shapes=[pltpu.VMEM((hp, tq, hp * HEAD_DIM), jnp.float32),
                        pltpu.VMEM((hp, tq, 1), jnp.float32)],
        compiler_params=pltpu.CompilerParams(dimension_semantics=("arbitrary", "arbitrary"),
                                             vmem_limit_bytes=VMEM_LIMIT),
        name="attn_prompt",
    )(qe, ke, vb, tri2)


PAGES_PER_CHUNK = 8
N_SLOTS = 3
NEW_PAD = LANES


def _attn_sample_kernel(pt_ref, q_ref, kn_ref, vn_ref, bias_ref, tri2_ref, ck_hbm, cv_hbm,
                        o_ref, kbuf, vbuf, sem, acc_ref, carry_ref, *, n_pages, page, tk):
    b = pl.program_id(0)
    nb = pl.num_programs(0)
    n_chunks = n_pages // PAGES_PER_CHUNK
    total = nb * n_chunks
    t_new = q_ref.shape[1]
    rows = t_new * N_HEADS
    d = q_ref.shape[2]
    pages_per_tile = tk // page

    def copies(g, slot):
        bb = g // n_chunks
        c = n_chunks - 1 - g % n_chunks
        out = []
        for p in range(PAGES_PER_CHUNK):
            pid = pt_ref[bb * n_pages + c * PAGES_PER_CHUNK + p]
            out.append(pltpu.make_async_copy(ck_hbm.at[pid], kbuf.at[slot, p], sem.at[slot, 0]))
            out.append(pltpu.make_async_copy(cv_hbm.at[pid], vbuf.at[slot, p], sem.at[slot, 1]))
        return out

    def start(g):
        for cp in copies(g, g % N_SLOTS):
            cp.start()

    @pl.when(b == 0)
    def _():
        for g0 in range(N_SLOTS - 1):
            start(g0)

    q4 = q_ref[0]
    qb = jnp.concatenate([jnp.broadcast_to(q4[t:t + 1, :], (N_HEADS, d)) for t in range(t_new)], axis=0)
    r_id = lax.broadcasted_iota(jnp.int32, (rows, d), 0)
    l_id = lax.broadcasted_iota(jnp.int32, (rows, d), 1)
    head_mask = (l_id // HEAD_DIM) == (r_id % N_HEADS)
    qb = jnp.where(head_mask, qb, 0.0).astype(jnp.bfloat16)
    bias = bias_ref[...][:, 0:1]

    def accumulate(a, total_sp, v, v_dims):
        pv = lax.dot_general(a, v, v_dims, preferred_element_type=jnp.float32)
        carry = carry_ref[...]
        acc_ref[...] += jnp.exp2(-carry) * pv
        carry_ref[...] = carry + total_sp

    acc_ref[...] = jnp.zeros_like(acc_ref)
    carry_ref[...] = jnp.zeros_like(carry_ref)

    pad = jnp.zeros((NEW_PAD - t_new, d), jnp.float32)
    kn = jnp.concatenate([kn_ref[0], pad], axis=0).astype(jnp.bfloat16)
    vn = jnp.concatenate([vn_ref[0], pad], axis=0).astype(jnp.bfloat16)
    z = lax.dot_general(qb, kn, _NT, preferred_element_type=jnp.float32) + bias
    rr = lax.broadcasted_iota(jnp.int32, (rows, NEW_PAD), 0)
    cc = lax.broadcasted_iota(jnp.int32, (rows, NEW_PAD), 1)
    tri_new = tri2_ref[:NEW_PAD, :NEW_PAD]
    a, tot = _sb_tile(z, jnp.concatenate([tri_new, tri_new], axis=0), cc < rr // N_HEADS)
    accumulate(a, tot, vn, _NN)

    def keys_on_lanes(buf, slot, s):
        first = s * pages_per_tile
        return jnp.concatenate([buf[slot, first + p] for p in range(pages_per_tile)],
                               axis=1).astype(jnp.bfloat16)

    def chunk(cidx, _):
        g = b * n_chunks + cidx
        slot = g % N_SLOTS
        for cp in copies(g, slot):
            cp.wait()

        @pl.when(g + N_SLOTS - 1 < total)
        def _():
            start(g + N_SLOTS - 1)

        tiles = list(reversed(range(PAGES_PER_CHUNK * page // tk)))
        zs = [jnp.dot(qb, keys_on_lanes(kbuf, slot, s), preferred_element_type=jnp.float32) + bias
              for s in tiles]
        splits = [_sb_split(_sb_masses(z, None)) for z in zs]
        sums = [_sb_suffix(hl, tri2_ref[...]) for hl in splits]
        weights = [_sb_weights(z, s_incl, None) for z, s_incl in zip(zs, sums)]
        pvs = [lax.dot_general(a, keys_on_lanes(vbuf, slot, s), _NT, preferred_element_type=jnp.float32)
               for a, s in zip(weights, tiles)]
        acc = acc_ref[...]
        carry = carry_ref[...]
        for pv, s_incl in zip(pvs, sums):
            acc = acc + jnp.exp2(-carry) * pv
            carry = carry + s_incl[:, 0:1]
        acc_ref[...] = acc
        carry_ref[...] = carry
        return 0

    lax.fori_loop(0, n_chunks, chunk, 0)

    o_full = jnp.where(head_mask, acc_ref[...], 0.0)
    o_ref[0] = jnp.sum(o_full.reshape(t_new, N_HEADS, d), axis=1).astype(o_ref.dtype)


def _attn_sample(page_table, q_s, kn_s, vn_s, bias_rows, tri2, cache_k, cache_v, tk):
    nb, t_new, d = q_s.shape
    n_pages = page_table.shape[1]
    n_pool, hd, page = cache_k.shape
    assert hd == d and tk % page == 0 and (PAGES_PER_CHUNK * page) % tk == 0
    rows = t_new * N_HEADS
    kern = functools.partial(_attn_sample_kernel, n_pages=n_pages, page=page, tk=tk)
    per_b = lambda shape: pl.BlockSpec(shape, lambda b, pt: (b, 0, 0))
    const = lambda shape: pl.BlockSpec(shape, lambda b, pt: (0,) * len(shape))
    grid_spec = pltpu.PrefetchScalarGridSpec(
        num_scalar_prefetch=1,
        grid=(nb,),
        in_specs=[per_b((1, t_new, d)), per_b((1, t_new, d)), per_b((1, t_new, d)),
                  const((rows, LANES)), const((2 * tk, tk)),
                  pl.BlockSpec(memory_space=pl.ANY), pl.BlockSpec(memory_space=pl.ANY)],
        out_specs=per_b((1, t_new, d)),
        scratch_shapes=[pltpu.VMEM((N_SLOTS, PAGES_PER_CHUNK, hd, page), jnp.float32),
                        pltpu.VMEM((N_SLOTS, PAGES_PER_CHUNK, hd, page), jnp.float32),
                        pltpu.SemaphoreType.DMA((N_SLOTS, 2)),
                        pltpu.VMEM((rows, d), jnp.float32),
                        pltpu.VMEM((rows, 1), jnp.float32)],
    )
    return pl.pallas_call(
        kern,
        grid_spec=grid_spec,
        out_shape=jax.ShapeDtypeStruct((nb, t_new, d), jnp.bfloat16),
        compiler_params=pltpu.CompilerParams(dimension_semantics=("arbitrary",),
                                             vmem_limit_bytes=VMEM_LIMIT),
        name="attn_sample",
    )(page_table.reshape(-1), q_s, kn_s, vn_s, bias_rows, tri2, cache_k, cache_v)


FF_CHUNK = 1024


def _post_kernel(x_ref, c_ref, o_ref, gc_ref, ga_ref, wc_ref, wa_ref, wo_ref, w1_ref, w2_ref,
                 g1_ref, g2_ref, g3_ref, y_ref):
    f32, bf16 = jnp.float32, jnp.bfloat16
    conv_out = jnp.dot(c_ref[...], wc_ref[...], preferred_element_type=f32)
    attn_out = jnp.dot(o_ref[...], wa_ref[...], preferred_element_type=f32)
    mix = gc_ref[...] * conv_out + ga_ref[...] * attn_out
    m = jnp.dot(mix.astype(bf16), wo_ref[...], preferred_element_type=f32)
    x1 = x_ref[...] + _rmsnorm(m, g1_ref[...])
    h2 = _rmsnorm(x1, g2_ref[...]).astype(bf16)
    f = jnp.zeros_like(x1)
    for c in range(0, w1_ref.shape[1], FF_CHUNK):
        hid = jnp.maximum(jnp.dot(h2, w1_ref[:, c:c + FF_CHUNK], preferred_element_type=f32), 0.0)
        f = f + jnp.dot((hid * hid).astype(bf16), w2_ref[c:c + FF_CHUNK, :], preferred_element_type=f32)
    y_ref[...] = x1 + _rmsnorm(f, g3_ref[...])


def _post(x, c, o, gc, ga, wc, wa, wo, w1, w2, g1, g2, g3, tm):
    rows, d_model = x.shape
    row = lambda n: pl.BlockSpec((tm, n), lambda i: (i, 0))
    return pl.pallas_call(
        _post_kernel,
        grid=(rows // tm,),
        in_specs=[row(d_model), row(c.shape[1]), row(o.shape[1]), row(d_model), row(d_model),
                  _const_spec(wc.shape), _const_spec(wa.shape), _const_spec(wo.shape),
                  _const_spec(w1.shape), _const_spec(w2.shape),
                  _const_spec((1, d_model)), _const_spec((1, d_model)), _const_spec((1, d_model))],
        out_specs=row(d_model),
        out_shape=jax.ShapeDtypeStruct((rows, d_model), jnp.float32),
        compiler_params=pltpu.CompilerParams(dimension_semantics=("arbitrary",),
                                             vmem_limit_bytes=VMEM_LIMIT),
        name="post",
    )(x, c, o, gc, ga, wc, wa, wo, w1, w2, g1, g2, g3)


def _head_slots(w):
    k = w.shape[0]
    w = w.reshape(k, N_HEADS, HEAD_DIM)
    w = jnp.pad(w, ((0, 0), (0, 0), (0, HEAD_SLOT - HEAD_DIM)))
    return w.reshape(k, N_HEADS * HEAD_SLOT)


def _bias_patterns(bias2):
    terms = []
    rem = bias2
    for _ in range(N_BIAS_COLS):
        t = rem.astype(jnp.bfloat16).astype(jnp.float32)
        terms.append(t)
        rem = rem - t
    qpat = jnp.zeros((N_HEADS, HEAD_SLOT), jnp.float32)
    kpat = jnp.zeros((N_HEADS, HEAD_SLOT), jnp.float32)
    for n, t in enumerate(terms):
        qpat = qpat.at[:, HEAD_DIM + n].set(t)
        kpat = kpat.at[:, HEAD_DIM + n].set(1.0)
    return qpat.reshape(1, -1), kpat.reshape(1, -1)


def _pick_tile(rows, pref):
    t = min(pref, rows)
    while rows % t:
        t //= 2
    return t


def kernel(x_prompt, x_sample, cache_k, cache_v, state_conv, page_table, g_pre_mix, w_in, sb_bias, w_dw,
           b_dw, ln_g, ln_b, w_conv_out, w_attn_out, w_o, g_post_mix, g_pre_ffn, w_ff1, w_ff2, g_post_ffn):
    depth = w_in.shape[0]
    assert depth == 1, "single-layer step"
    f32, bf16 = jnp.float32, jnp.bfloat16
    batch, seq, d_model = x_prompt.shape
    assert batch == 1
    dec_batch, dec_seq, _ = x_sample.shape
    d_conv = d_model // 2
    q_scale = LOG2E / math.sqrt(HEAD_DIM)

    w = w_in[0]
    o_q = 2 * d_conv
    w_q = w[:, o_q:o_q + D_ATTN]
    w_k = w[:, o_q + D_ATTN:o_q + 2 * D_ATTN]
    w_all = jnp.concatenate(
        [w[:, :o_q], _head_slots(w_q * q_scale), _head_slots(w_k),
         w[:, o_q + D_ATTN:o_q + 3 * D_ATTN], w[:, o_q + 3 * D_ATTN:]], axis=1).astype(bf16)
    bias2 = sb_bias[0].astype(f32) * LOG2E
    qpat, kpat = _bias_patterns(bias2)
    row1 = lambda a: a[0].reshape(1, -1).astype(f32)
    g0, g1, g2, g3 = row1(g_pre_mix), row1(g_post_mix), row1(g_pre_ffn), row1(g_post_ffn)
    wdw = w_dw[0].astype(f32)
    bdw, lng, lnb = row1(b_dw), row1(ln_g), row1(ln_b)
    wc, wa, wo = w_conv_out[0].astype(bf16), w_attn_out[0].astype(bf16), w_o[0].astype(bf16)
    w1, w2 = w_ff1[0].astype(bf16), w_ff2[0].astype(bf16)

    tk = 256
    jj = lax.broadcasted_iota(jnp.int32, (tk, tk), 0)
    ss = lax.broadcasted_iota(jnp.int32, (tk, tk), 1)
    tri = (jj >= ss).astype(bf16)
    tri2 = jnp.concatenate([tri, tri], axis=0)

    xp = x_prompt[0]
    tm = _pick_tile(seq, 256)
    u_p, qe_p, ke_p, k_p, v_p, vb_p, gc_p, ga_p = _proj(xp, g0, w_all, qpat, kpat, tm)
    c_p = _conv_prompt(u_p, wdw, bdw, lng, lnb, _pick_tile(seq, 512))
    o_p = _attn_prompt(qe_p, ke_p, vb_p, tri2, tq=_pick_tile(seq, 512), tk=tk)
    y_p = _post(xp, c_p, o_p, gc_p, ga_p, wc, wa, wo, w1, w2, g1, g2, g3, tm)

    xs = x_sample.reshape(dec_batch * dec_seq, d_model)
    tms = _pick_tile(dec_batch * dec_seq, 256)
    u_s, qe_s, ke_s, k_s, v_s, vb_s, gc_s, ga_s = _proj(xs, g0, w_all, qpat, kpat, tms)
    state_t = jnp.transpose(state_conv[0], (1, 0, 2))
    u_t = jnp.transpose(u_s.reshape(dec_batch, dec_seq, d_conv), (1, 0, 2))
    c_t = _conv_sample(state_t, u_t, wdw, bdw, lng, lnb)
    c_s = jnp.transpose(c_t, (1, 0, 2)).reshape(dec_batch * dec_seq, d_conv)
    unslot = lambda a: a.reshape(-1, N_HEADS, HEAD_SLOT)[:, :, :HEAD_DIM].reshape(dec_batch, dec_seq, D_ATTN)
    q_s = unslot(qe_s).astype(f32)
    kn_s = unslot(ke_s).astype(f32)
    vn_s = vb_s.reshape(dec_batch, dec_seq, D_ATTN).astype(f32)
    bias_rows = jnp.broadcast_to(jnp.tile(bias2, dec_seq)[:, None], (dec_seq * N_HEADS, LANES))
    n_pool, page = cache_k.shape[1], cache_k.shape[2]
    pages_t = lambda c: jnp.transpose(c[0], (0, 2, 3, 1)).reshape(n_pool, D_ATTN, page)
    o_s = _attn_sample(page_table, q_s, kn_s, vn_s, bias_rows, tri2, pages_t(cache_k), pages_t(cache_v), tk)
    y_s = _post(xs, c_s, o_s.reshape(dec_batch * dec_seq, D_ATTN), gc_s, ga_s,
                wc, wa, wo, w1, w2, g1, g2, g3, tms)

    n_state = CONV_WIDTH - 1
    y_prompt = y_p.reshape(batch, seq, d_model)
    y_sample = y_s.reshape(dec_batch, dec_seq, d_model)
    k_prompt = k_p.reshape(1, batch, seq, N_HEADS, HEAD_DIM)
    v_prompt = v_p.reshape(1, batch, seq, N_HEADS, HEAD_DIM)
    conv_prompt = u_p[seq - n_state:].reshape(1, batch, n_state, d_conv)
    k_sample = k_s.reshape(1, dec_batch, dec_seq, N_HEADS, HEAD_DIM)
    v_sample = v_s.reshape(1, dec_batch, dec_seq, N_HEADS, HEAD_DIM)
    conv_sample = jnp.concatenate(
        [state_conv[0][:, dec_seq:, :], u_s.reshape(dec_batch, dec_seq, d_conv)], axis=1)[None]
    return (y_prompt, y_sample, k_prompt, v_prompt, conv_prompt, k_sample, v_sample, conv_sample)
```

```python
import functools
import math

import jax
import jax.numpy as jnp
from jax import lax
from jax.experimental import pallas as pl
from jax.experimental.pallas import tpu as pltpu

N_HEADS = 8
HEAD_DIM = 64
D_ATTN = N_HEADS * HEAD_DIM
CONV_WIDTH = 31
NORM_EPS = 1e-6
LOG2E = math.log2(math.e)

LANES = 128
HEAD_SLOT = LANES
N_BIAS_COLS = 3
VMEM_LIMIT = 56 * 1024 * 1024

_NT = (((1,), (1,)), ((), ()))
_NN = (((1,), (0,)), ((), ()))


def _rmsnorm(x, g):
    return x * lax.rsqrt(jnp.mean(x * x, axis=-1, keepdims=True) + NORM_EPS) * g


def _softplus2(z):
    sign_bit = jnp.uint32(0x80000000)
    neg_abs = lax.bitcast_convert_type(lax.bitcast_convert_type(z, jnp.uint32) | sign_bit, jnp.float32)
    return jnp.maximum(z, 0.0) + jnp.log2(1.0 + jnp.exp2(neg_abs))


def _const_spec(shape):
    return pl.BlockSpec(shape, lambda *_: (0,) * len(shape), pipeline_mode=pl.Buffered(1))


def _proj_kernel(x_ref, g_ref, w_ref, qpat_ref, kpat_ref,
                 u_ref, qe_ref, ke_ref, k_ref, v_ref, vb_ref, gc_ref, ga_ref):
    d_conv = u_ref.shape[1]
    d_model = x_ref.shape[1]
    h = _rmsnorm(x_ref[...], g_ref[...]).astype(jnp.bfloat16)

    def mm(lo, width):
        return jnp.dot(h, w_ref[:, lo:lo + width], preferred_element_type=jnp.float32)

    r = mm(0, 2 * d_conv)
    u_ref[...] = r[:, :d_conv] * jax.nn.sigmoid(r[:, d_conv:])
    off = 2 * d_conv
    qe_ref[...] = (mm(off, N_HEADS * HEAD_SLOT) + qpat_ref[...]).astype(jnp.bfloat16)
    off += N_HEADS * HEAD_SLOT
    ke_ref[...] = (mm(off, N_HEADS * HEAD_SLOT) + kpat_ref[...]).astype(jnp.bfloat16)
    off += N_HEADS * HEAD_SLOT
    r = mm(off, 2 * D_ATTN)
    k_ref[...] = r[:, :D_ATTN]
    v_ref[...] = r[:, D_ATTN:]
    vb_ref[...] = r[:, D_ATTN:].astype(jnp.bfloat16)
    off += 2 * D_ATTN
    gc_ref[...] = jax.nn.sigmoid(mm(off, d_model))
    ga_ref[...] = jax.nn.sigmoid(mm(off + d_model, d_model))


def _proj(x, g, w_all, qpat, kpat, tm):
    rows, d_model = x.shape
    d_conv = d_model // 2
    n_w = w_all.shape[1]
    row = lambda n: pl.BlockSpec((tm, n), lambda i: (i, 0))
    f32, bf16 = jnp.float32, jnp.bfloat16
    return pl.pallas_call(
        _proj_kernel,
        grid=(rows // tm,),
        in_specs=[row(d_model), _const_spec((1, d_model)), _const_spec((d_model, n_w)),
                  _const_spec((1, N_HEADS * HEAD_SLOT)), _const_spec((1, N_HEADS * HEAD_SLOT))],
        out_specs=[row(d_conv), row(N_HEADS * HEAD_SLOT), row(N_HEADS * HEAD_SLOT),
                   row(D_ATTN), row(D_ATTN), row(D_ATTN), row(d_model), row(d_model)],
        out_shape=[jax.ShapeDtypeStruct((rows, d_conv), f32),
                   jax.ShapeDtypeStruct((rows, N_HEADS * HEAD_SLOT), bf16),
                   jax.ShapeDtypeStruct((rows, N_HEADS * HEAD_SLOT), bf16),
                   jax.ShapeDtypeStruct((rows, D_ATTN), f32),
                   jax.ShapeDtypeStruct((rows, D_ATTN), f32),
                   jax.ShapeDtypeStruct((rows, D_ATTN), bf16),
                   jax.ShapeDtypeStruct((rows, d_model), f32),
                   jax.ShapeDtypeStruct((rows, d_model), f32)],
        compiler_params=pltpu.CompilerParams(dimension_semantics=("arbitrary",),
                                             vmem_limit_bytes=VMEM_LIMIT),
        name="proj",
    )(x, g, w_all, qpat, kpat)


def _ln_silu(y, g, b):
    mu = jnp.mean(y, axis=-1, keepdims=True)
    d = y - mu
    var = jnp.mean(d * d, axis=-1, keepdims=True)
    n = d * lax.rsqrt(var + NORM_EPS) * g + b
    return n * jax.nn.sigmoid(n)


HALO = 32


def _conv_prompt_kernel(prev_ref, cur_ref, w_ref, b_ref, g_ref, lb_ref, c_ref, full_ref):
    tm = cur_ref.shape[0]
    i = pl.program_id(0)
    full_ref[0:HALO, :] = jnp.where(i > 0, prev_ref[...], 0.0)
    full_ref[HALO:, :] = cur_ref[...]
    first = HALO - (CONV_WIDTH - 1)
    acc = jnp.broadcast_to(b_ref[...], (tm, cur_ref.shape[1]))
    for j in range(CONV_WIDTH):
        acc = acc + full_ref[first + j:first + j + tm, :] * w_ref[j:j + 1, :]
    c_ref[...] = _ln_silu(acc, g_ref[...], lb_ref[...]).astype(jnp.bfloat16)


def _conv_prompt(u, w_dw, b_dw, ln_g, ln_b, tm):
    rows, d = u.shape
    per = tm // HALO
    return pl.pallas_call(
        _conv_prompt_kernel,
        grid=(rows // tm,),
        in_specs=[pl.BlockSpec((HALO, d), lambda i: (jnp.maximum(i * per - 1, 0), 0)),
                  pl.BlockSpec((tm, d), lambda i: (i, 0)),
                  _const_spec(w_dw.shape), _const_spec((1, d)), _const_spec((1, d)), _const_spec((1, d))],
        out_specs=pl.BlockSpec((tm, d), lambda i: (i, 0)),
        out_shape=jax.ShapeDtypeStruct((rows, d), jnp.bfloat16),
        scratch_shapes=[pltpu.VMEM((tm + HALO, d), jnp.float32)],
        compiler_params=pltpu.CompilerParams(dimension_semantics=("arbitrary",)),
        name="conv_prompt",
    )(u, u, w_dw, b_dw, ln_g, ln_b)


def _conv_sample_kernel(state_ref, u_ref, w_ref, b_ref, g_ref, lb_ref, c_ref):
    n_state = state_ref.shape[0]
    t_new, nb, d = u_ref.shape
    for t in range(t_new):
        acc = jnp.broadcast_to(b_ref[...], (nb, d))
        for j in range(CONV_WIDTH):
            r = t + j
            src = state_ref[r] if r < n_state else u_ref[r - n_state]
            acc = acc + src * w_ref[j:j + 1, :]
        c_ref[t] = _ln_silu(acc, g_ref[...], lb_ref[...]).astype(jnp.bfloat16)


def _conv_sample(state_t, u_t, w_dw, b_dw, ln_g, ln_b):
    t_new, nb, d = u_t.shape
    return pl.pallas_call(
        _conv_sample_kernel,
        out_shape=jax.ShapeDtypeStruct((t_new, nb, d), jnp.bfloat16),
        compiler_params=pltpu.CompilerParams(vmem_limit_bytes=VMEM_LIMIT),
        name="conv_sample",
    )(state_t, u_t, w_dw, b_dw, ln_g, ln_b)


def _sb_masses(z, valid):
    sp = _softplus2(z)
    return sp if valid is None else jnp.where(valid, sp, 0.0)


def _sb_split(sp):
    return sp.astype(jnp.bfloat16)


def _sb_suffix(sp_bf16, tri):
    return jnp.dot(sp_bf16, tri, preferred_element_type=jnp.float32)


def _sb_weights(z, s_incl, valid):
    log_a = z - s_incl
    if valid is not None:
        log_a = jnp.where(valid, log_a, -jnp.inf)
    return jnp.exp2(log_a).astype(jnp.bfloat16)


def _sb_tile(z, tri, valid):
    s_incl = _sb_suffix(_sb_split(_sb_masses(z, valid)), tri)
    return _sb_weights(z, s_incl, valid), s_incl[:, 0:1]


TILES_PER_STEP = 2


def _attn_prompt_kernel(q_ref, k_ref, v_ref, tri_ref, o_ref, acc_ref, carry_ref, *, tq, tk):
    i = pl.program_id(1)
    n_diag = tq // tk
    heads = q_ref.shape[1] // HEAD_SLOT
    acc_ref[...] = jnp.zeros_like(acc_ref)
    carry_ref[...] = jnp.zeros_like(carry_ref)

    def step(js, valids):
        pairs = [(hh, j, valid) for j, valid in zip(js, valids) for hh in range(heads)]
        lanes = lambda hh: slice(hh * HEAD_SLOT, (hh + 1) * HEAD_SLOT)
        keys = lambda j: pl.ds(pl.multiple_of(j * tk, tk), tk)
        zs = [lax.dot_general(q_ref[:, lanes(hh)], k_ref[keys(j), lanes(hh)], _NT,
                              preferred_element_type=jnp.float32) for hh, j, _ in pairs]
        sps = [_sb_masses(z, valid) for z, (_, _, valid) in zip(zs, pairs)]
        splits = [_sb_split(sp) for sp in sps]
        sums = [_sb_suffix(hl, tri_ref[...]) for hl in splits]
        weights = [_sb_weights(z, s, valid) for z, s, (_, _, valid) in zip(zs, sums, pairs)]
        pvs = [jnp.dot(a, v_ref[keys(j), :], preferred_element_type=jnp.float32)
               for a, (_, j, _) in zip(weights, pairs)]
        for hh in range(heads):
            acc = acc_ref[hh]
            carry = carry_ref[hh]
            for n, (h2, _, _) in enumerate(pairs):
                if h2 == hh:
                    acc = acc + jnp.exp2(-carry) * pvs[n]
                    carry = carry + sums[n][:, 0:1]
            acc_ref[hh] = acc
            carry_ref[hh] = carry

    row = lax.broadcasted_iota(jnp.int32, (tq, tk), 0)
    col = lax.broadcasted_iota(jnp.int32, (tq, tk), 1)
    diag = list(reversed(range(n_diag)))
    step([i * n_diag + d for d in diag], [col + d * tk < row for d in diag])

    n_full = i * n_diag
    assert n_diag % TILES_PER_STEP == 0

    def body(s, _):
        first = n_full - 1 - s * TILES_PER_STEP
        step([first - u for u in range(TILES_PER_STEP)], [None] * TILES_PER_STEP)
        return 0

    lax.fori_loop(0, n_full // TILES_PER_STEP, body, 0)

    lane = lax.broadcasted_iota(jnp.int32, (tq, heads * HEAD_DIM), 1)
    o = acc_ref[0]
    for hh in range(1, heads):
        o = jnp.where(lane >= hh * HEAD_DIM, acc_ref[hh], o)
    o_ref[...] = o.astype(o_ref.dtype)


def _attn_prompt(qe, ke, vb, tri, tq, tk):
    t = qe.shape[0]
    hp = 2
    n_groups = N_HEADS // hp
    kern = functools.partial(_attn_prompt_kernel, tq=tq, tk=tk)
    return pl.pallas_call(
        kern,
        grid=(n_groups, t // tq),
        in_specs=[pl.BlockSpec((tq, hp * HEAD_SLOT), lambda g, i: (i, g)),
                  pl.BlockSpec((t, hp * HEAD_SLOT), lambda g, i: (0, g)),
                  pl.BlockSpec((t, hp * HEAD_DIM), lambda g, i: (0, g)),
                  _const_spec((tk, tk))],
        out_specs=pl.BlockSpec((tq, hp * HEAD_DIM), lambda g, i: (i, g)),
        out_shape=jax.ShapeDtypeStruct((t, D_ATTN), jnp.bfloat16),
        scratch_shapes=[pltpu.VMEM((hp, tq, hp * HEAD_DIM), jnp.float32),
                        pltpu.VMEM((hp, tq, 1), jnp.float32)],
        compiler_params=pltpu.CompilerParams(dimension_semantics=("arbitrary", "arbitrary"),
                                             vmem_limit_bytes=VMEM_LIMIT),
        name="attn_prompt",
    )(qe, ke, vb, tri)


PAGES_PER_CHUNK = 8
N_SLOTS = 3
NEW_PAD = LANES


def _attn_sample_kernel(pt_ref, q_ref, kn_ref, vn_ref, bias_ref, tri_ref, ck_hbm, cv_hbm,
                        o_ref, kbuf, vbuf, sem, acc_ref, carry_ref, *, n_pages, page, tk):
    b = pl.program_id(0)
    nb = pl.num_programs(0)
    n_chunks = n_pages // PAGES_PER_CHUNK
    total = nb * n_chunks
    t_new = q_ref.shape[1]
    rows = t_new * N_HEADS
    d = q_ref.shape[2]
    pages_per_tile = tk // page

    def copies(g, slot):
        bb = g // n_chunks
        c = n_chunks - 1 - g % n_chunks
        out = []
        for p in range(PAGES_PER_CHUNK):
            pid = pt_ref[bb * n_pages + c * PAGES_PER_CHUNK + p]
            out.append(pltpu.make_async_copy(ck_hbm.at[pid], kbuf.at[slot, p], sem.at[slot, 0]))
            out.append(pltpu.make_async_copy(cv_hbm.at[pid], vbuf.at[slot, p], sem.at[slot, 1]))
        return out

    def start(g):
        for cp in copies(g, g % N_SLOTS):
            cp.start()

    @pl.when(b == 0)
    def _():
        for g0 in range(N_SLOTS - 1):
            start(g0)

    q4 = q_ref[0]
    qb = jnp.concatenate([jnp.broadcast_to(q4[t:t + 1, :], (N_HEADS, d)) for t in range(t_new)], axis=0)
    r_id = lax.broadcasted_iota(jnp.int32, (rows, d), 0)
    l_id = lax.broadcasted_iota(jnp.int32, (rows, d), 1)
    head_mask = (l_id // HEAD_DIM) == (r_id % N_HEADS)
    qb = jnp.where(head_mask, qb, 0.0).astype(jnp.bfloat16)
    bias = bias_ref[...][:, 0:1]

    def accumulate(a, total_sp, v, v_dims):
        pv = lax.dot_general(a, v, v_dims, preferred_element_type=jnp.float32)
        carry = carry_ref[...]
        acc_ref[...] += jnp.exp2(-carry) * pv
        carry_ref[...] = carry + total_sp

    acc_ref[...] = jnp.zeros_like(acc_ref)
    carry_ref[...] = jnp.zeros_like(carry_ref)

    pad = jnp.zeros((NEW_PAD - t_new, d), jnp.float32)
    kn = jnp.concatenate([kn_ref[0], pad], axis=0).astype(jnp.bfloat16)
    vn = jnp.concatenate([vn_ref[0], pad], axis=0).astype(jnp.bfloat16)
    z = lax.dot_general(qb, kn, _NT, preferred_element_type=jnp.float32) + bias
    rr = lax.broadcasted_iota(jnp.int32, (rows, NEW_PAD), 0)
    cc = lax.broadcasted_iota(jnp.int32, (rows, NEW_PAD), 1)
    a, tot = _sb_tile(z, tri_ref[:NEW_PAD, :NEW_PAD], cc < rr // N_HEADS)
    accumulate(a, tot, vn, _NN)

    def keys_on_lanes(buf, slot, s):
        first = s * pages_per_tile
        return jnp.concatenate([buf[slot, first + p] for p in range(pages_per_tile)],
                               axis=1).astype(jnp.bfloat16)

    def chunk(cidx, _):
        g = b * n_chunks + cidx
        slot = g % N_SLOTS
        for cp in copies(g, slot):
            cp.wait()

        @pl.when(g + N_SLOTS - 1 < total)
        def _():
            start(g + N_SLOTS - 1)

        tiles = list(reversed(range(PAGES_PER_CHUNK * page // tk)))
        zs = [jnp.dot(qb, keys_on_lanes(kbuf, slot, s), preferred_element_type=jnp.float32) + bias
              for s in tiles]
        splits = [_sb_split(_sb_masses(z, None)) for z in zs]
        sums = [_sb_suffix(hl, tri_ref[...]) for hl in splits]
        weights = [_sb_weights(z, s_incl, None) for z, s_incl in zip(zs, sums)]
        pvs = [lax.dot_general(a, keys_on_lanes(vbuf, slot, s), _NT, preferred_element_type=jnp.float32)
               for a, s in zip(weights, tiles)]
        acc = acc_ref[...]
        carry = carry_ref[...]
        for pv, s_incl in zip(pvs, sums):
            acc = acc + jnp.exp2(-carry) * pv
            carry = carry + s_incl[:, 0:1]
        acc_ref[...] = acc
        carry_ref[...] = carry
        return 0

    lax.fori_loop(0, n_chunks, chunk, 0)

    o_full = jnp.where(head_mask, acc_ref[...], 0.0)
    o_ref[0] = jnp.sum(o_full.reshape(t_new, N_HEADS, d), axis=1).astype(o_ref.dtype)


def _attn_sample(page_table, q_s, kn_s, vn_s, bias_rows, tri, cache_k, cache_v, tk):
    nb, t_new, d = q_s.shape
    n_pages = page_table.shape[1]
    n_pool, hd, page = cache_k.shape
    assert hd == d and tk % page == 0 and (PAGES_PER_CHUNK * page) % tk == 0
    rows = t_new * N_HEADS
    kern = functools.partial(_attn_sample_kernel, n_pages=n_pages, page=page, tk=tk)
    per_b = lambda shape: pl.BlockSpec(shape, lambda b, pt: (b, 0, 0))
    const = lambda shape: pl.BlockSpec(shape, lambda b, pt: (0,) * len(shape))
    grid_spec = pltpu.PrefetchScalarGridSpec(
        num_scalar_prefetch=1,
        grid=(nb,),
        in_specs=[per_b((1, t_new, d)), per_b((1, t_new, d)), per_b((1, t_new, d)),
                  const((rows, LANES)), const((tk, tk)),
                  pl.BlockSpec(memory_space=pl.ANY), pl.BlockSpec(memory_space=pl.ANY)],
        out_specs=per_b((1, t_new, d)),
        scratch_shapes=[pltpu.VMEM((N_SLOTS, PAGES_PER_CHUNK, hd, page), jnp.float32),
                        pltpu.VMEM((N_SLOTS, PAGES_PER_CHUNK, hd, page), jnp.float32),
                        pltpu.SemaphoreType.DMA((N_SLOTS, 2)),
                        pltpu.VMEM((rows, d), jnp.float32),
                        pltpu.VMEM((rows, 1), jnp.float32)],
    )
    return pl.pallas_call(
        kern,
        grid_spec=grid_spec,
        out_shape=jax.ShapeDtypeStruct((nb, t_new, d), jnp.bfloat16),
        compiler_params=pltpu.CompilerParams(dimension_semantics=("arbitrary",),
                                             vmem_limit_bytes=VMEM_LIMIT),
        name="attn_sample",
    )(page_table.reshape(-1), q_s, kn_s, vn_s, bias_rows, tri, cache_k, cache_v)


FF_CHUNK = 1024


def _post_kernel(x_ref, c_ref, o_ref, gc_ref, ga_ref, wc_ref, wa_ref, wo_ref, w1_ref, w2_ref,
                 g1_ref, g2_ref, g3_ref, y_ref):
    f32, bf16 = jnp.float32, jnp.bfloat16
    conv_out = jnp.dot(c_ref[...], wc_ref[...], preferred_element_type=f32)
    attn_out = jnp.dot(o_ref[...], wa_ref[...], preferred_element_type=f32)
    mix = gc_ref[...] * conv_out + ga_ref[...] * attn_out
    m = jnp.dot(mix.astype(bf16), wo_ref[...], preferred_element_type=f32)
    x1 = x_ref[...] + _rmsnorm(m, g1_ref[...])
    h2 = _rmsnorm(x1, g2_ref[...]).astype(bf16)
    f = jnp.zeros_like(x1)
    for c in range(0, w1_ref.shape[1], FF_CHUNK):
        hid = jnp.maximum(jnp.dot(h2, w1_ref[:, c:c + FF_CHUNK], preferred_element_type=f32), 0.0)
        f = f + jnp.dot((hid * hid).astype(bf16), w2_ref[c:c + FF_CHUNK, :], preferred_element_type=f32)
    y_ref[...] = x1 + _rmsnorm(f, g3_ref[...])


def _post(x, c, o, gc, ga, wc, wa, wo, w1, w2, g1, g2, g3, tm):
    rows, d_model = x.shape
    row = lambda n: pl.BlockSpec((tm, n), lambda i: (i, 0))
    return pl.pallas_call(
        _post_kernel,
        grid=(rows // tm,),
        in_specs=[row(d_model), row(c.shape[1]), row(o.shape[1]), row(d_model), row(d_model),
                  _const_spec(wc.shape), _const_spec(wa.shape), _const_spec(wo.shape),
                  _const_spec(w1.shape), _const_spec(w2.shape),
                  _const_spec((1, d_model)), _const_spec((1, d_model)), _const_spec((1, d_model))],
        out_specs=row(d_model),
        out_shape=jax.ShapeDtypeStruct((rows, d_model), jnp.float32),
        compiler_params=pltpu.CompilerParams(dimension_semantics=("arbitrary",),
                                             vmem_limit_bytes=VMEM_LIMIT),
        name="post",
    )(x, c, o, gc, ga, wc, wa, wo, w1, w2, g1, g2, g3)


def _head_slots(w):
    k = w.shape[0]
    w = w.reshape(k, N_HEADS, HEAD_DIM)
    w = jnp.pad(w, ((0, 0), (0, 0), (0, HEAD_SLOT - HEAD_DIM)))
    return w.reshape(k, N_HEADS * HEAD_SLOT)


def _bias_patterns(bias2):
    terms = []
    rem = bias2
    for _ in range(N_BIAS_COLS):
        t = rem.astype(jnp.bfloat16).astype(jnp.float32)
        terms.append(t)
        rem = rem - t
    qpat = jnp.zeros((N_HEADS, HEAD_SLOT), jnp.float32)
    kpat = jnp.zeros((N_HEADS, HEAD_SLOT), jnp.float32)
    for n, t in enumerate(terms):
        qpat = qpat.at[:, HEAD_DIM + n].set(t)
        kpat = kpat.at[:, HEAD_DIM + n].set(1.0)
    return qpat.reshape(1, -1), kpat.reshape(1, -1)


def _pick_tile(rows, pref):
    t = min(pref, rows)
    while rows % t:
        t //= 2
    return t


def kernel(x_prompt, x_sample, cache_k, cache_v, state_conv, page_table, g_pre_mix, w_in, sb_bias, w_dw,
           b_dw, ln_g, ln_b, w_conv_out, w_attn_out, w_o, g_post_mix, g_pre_ffn, w_ff1, w_ff2, g_post_ffn):
    depth = w_in.shape[0]
    assert depth == 1, "single-layer step"
    f32, bf16 = jnp.float32, jnp.bfloat16
    batch, seq, d_model = x_prompt.shape
    assert batch == 1
    dec_batch, dec_seq, _ = x_sample.shape
    d_conv = d_model // 2
    q_scale = LOG2E / math.sqrt(HEAD_DIM)

    w = w_in[0]
    o_q = 2 * d_conv
    w_q = w[:, o_q:o_q + D_ATTN]
    w_k = w[:, o_q + D_ATTN:o_q + 2 * D_ATTN]
    w_all = jnp.concatenate(
        [w[:, :o_q], _head_slots(w_q * q_scale), _head_slots(w_k),
         w[:, o_q + D_ATTN:o_q + 3 * D_ATTN], w[:, o_q + 3 * D_ATTN:]], axis=1).astype(bf16)
    bias2 = sb_bias[0].astype(f32) * LOG2E
    qpat, kpat = _bias_patterns(bias2)
    row1 = lambda a: a[0].reshape(1, -1).astype(f32)
    g0, g1, g2, g3 = row1(g_pre_mix), row1(g_post_mix), row1(g_pre_ffn), row1(g_post_ffn)
    wdw = w_dw[0].astype(f32)
    bdw, lng, lnb = row1(b_dw), row1(ln_g), row1(ln_b)
    wc, wa, wo = w_conv_out[0].astype(bf16), w_attn_out[0].astype(bf16), w_o[0].astype(bf16)
    w1, w2 = w_ff1[0].astype(bf16), w_ff2[0].astype(bf16)

    tk = 256
    jj = lax.broadcasted_iota(jnp.int32, (tk, tk), 0)
    ss = lax.broadcasted_iota(jnp.int32, (tk, tk), 1)
    tri = (jj >= ss).astype(bf16)

    xp = x_prompt[0]
    tm = _pick_tile(seq, 256)
    u_p, qe_p, ke_p, k_p, v_p, vb_p, gc_p, ga_p = _proj(xp, g0, w_all, qpat, kpat, tm)
    c_p = _conv_prompt(u_p, wdw, bdw, lng, lnb, _pick_tile(seq, 512))
    o_p = _attn_prompt(qe_p, ke_p, vb_p, tri, tq=_pick_tile(seq, 512), tk=tk)
    y_p = _post(xp, c_p, o_p, gc_p, ga_p, wc, wa, wo, w1, w2, g1, g2, g3, tm)

    xs = x_sample.reshape(dec_batch * dec_seq, d_model)
    tms = _pick_tile(dec_batch * dec_seq, 256)
    u_s, qe_s, ke_s, k_s, v_s, vb_s, gc_s, ga_s = _proj(xs, g0, w_all, qpat, kpat, tms)
    state_t = jnp.transpose(state_conv[0], (1, 0, 2))
    u_t = jnp.transpose(u_s.reshape(dec_batch, dec_seq, d_conv), (1, 0, 2))
    c_t = _conv_sample(state_t, u_t, wdw, bdw, lng, lnb)
    c_s = jnp.transpose(c_t, (1, 0, 2)).reshape(dec_batch * dec_seq, d_conv)
    unslot = lambda a: a.reshape(-1, N_HEADS, HEAD_SLOT)[:, :, :HEAD_DIM].reshape(dec_batch, dec_seq, D_ATTN)
    q_s = unslot(qe_s).astype(f32)
    kn_s = unslot(ke_s).astype(f32)
    vn_s = vb_s.reshape(dec_batch, dec_seq, D_ATTN).astype(f32)
    bias_rows = jnp.broadcast_to(jnp.tile(bias2, dec_seq)[:, None], (dec_seq * N_HEADS, LANES))
    n_pool, page = cache_k.shape[1], cache_k.shape[2]
    pages_t = lambda c: jnp.transpose(c[0], (0, 2, 3, 1)).reshape(n_pool, D_ATTN, page)
    o_s = _attn_sample(page_table, q_s, kn_s, vn_s, bias_rows, tri, pages_t(cache_k), pages_t(cache_v), tk)
    y_s = _post(xs, c_s, o_s.reshape(dec_batch * dec_seq, D_ATTN), gc_s, ga_s,
                wc, wa, wo, w1, w2, g1, g2, g3, tms)

    n_state = CONV_WIDTH - 1
    y_prompt = y_p.reshape(batch, seq, d_model)
    y_sample = y_s.reshape(dec_batch, dec_seq, d_model)
    k_prompt = k_p.reshape(1, batch, seq, N_HEADS, HEAD_DIM)
    v_prompt = v_p.reshape(1, batch, seq, N_HEADS, HEAD_DIM)
    conv_prompt = u_p[seq - n_state:].reshape(1, batch, n_state, d_conv)
    k_sample = k_s.reshape(1, dec_batch, dec_seq, N_HEADS, HEAD_DIM)
    v_sample = v_s.reshape(1, dec_batch, dec_seq, N_HEADS, HEAD_DIM)
    conv_sample = jnp.concatenate(
        [state_conv[0][:, dec_seq:, :], u_s.reshape(dec_batch, dec_seq, d_conv)], axis=1)[None]
    return (y_prompt, y_sample, k_prompt, v_prompt, conv_prompt, k_sample, v_sample, conv_sample)
```
